```python
import math
import jax, jax.numpy as jnp
from jax import lax
import numpy as np

D_MODEL = 4096
BATCH = 2
SEQ = 8192
DEPTH = 1

MIX_WIDTH = D_MODEL
DN_WIDTH = MIX_WIDTH // 2
DN_HEAD_DIM = 128
DN_HEADS = DN_WIDTH // DN_HEAD_DIM
DN_CHUNK = 64
CONV_WIDTH = 4
SSM_WIDTH = MIX_WIDTH - DN_WIDTH
SSM_GROUP = 16
SSM_GROUPS = SSM_WIDTH // SSM_GROUP
SSM_STATE = 64
SSM_CHUNK = 128
QKV_END = 3 * DN_WIDTH
Z_END = 4 * DN_WIDTH
BETA_END = Z_END + DN_HEADS
ALPHA_END = BETA_END + DN_HEADS
IN_COLS = ALPHA_END + SSM_WIDTH
PEER_HEADS = 8
PEER_N_KEYS = 128
PEER_EXPERTS = PEER_N_KEYS * PEER_N_KEYS
PEER_QUERY_DIM = 256
PEER_HALF = PEER_QUERY_DIM // 2
PEER_TOPK = 16
PEER_TOKEN_BLOCK = 128
NORM_EPS = 1e-6

kernel_name = "hybrid_deltanet_s5_peer_layer"


def rms_norm(x, w):
    xf = x.astype(jnp.float32)
    y = xf * lax.rsqrt(jnp.mean(xf * xf, axis=-1, keepdims=True) + NORM_EPS)
    return (y * w.astype(jnp.float32)).astype(x.dtype)


def l2_normalize(x):
    return x * lax.rsqrt(jnp.sum(x * x, axis=-1, keepdims=True) + NORM_EPS)


def causal_depthwise_conv(x, w):
    c = x.shape[-1]
    return lax.conv_general_dilated(
        x, w[:, None, :].astype(x.dtype), window_strides=(1,),
        padding=((CONV_WIDTH - 1, 0),),
        dimension_numbers=("NWC", "WIO", "NWC"), feature_group_count=c)


def chunk_gated_delta_rule(q, k, v, g, beta):
    bsz, length, nh, dk = q.shape
    dv = v.shape[-1]
    n = length // DN_CHUNK
    def chunks(t):
        return t.reshape(bsz, n, DN_CHUNK, nh, t.shape[-1]).transpose(0, 3, 1, 2, 4)
    qc, kc, vc = chunks(q), chunks(k), chunks(v)
    gc = jnp.cumsum(g.reshape(bsz, n, DN_CHUNK, nh).transpose(0, 3, 1, 2), axis=-1)
    bc = beta.reshape(bsz, n, DN_CHUNK, nh).transpose(0, 3, 1, 2)
    k_beta = kc * bc[..., None]
    v_beta = vc * bc[..., None]
    causal = jnp.tril(jnp.ones((DN_CHUNK, DN_CHUNK), dtype=bool))
    strict = jnp.tril(jnp.ones((DN_CHUNK, DN_CHUNK), dtype=bool), -1)
    diff = gc[..., :, None] - gc[..., None, :]
    decay = jnp.where(causal, jnp.exp(jnp.where(causal, diff, 0.0)), 0.0)
    lower = jnp.where(strict, jnp.einsum("bhncd,bhnsd->bhncs", k_beta, kc) * decay, 0.0)
    a_mat = lower + jnp.eye(DN_CHUNK, dtype=jnp.float32)
    u_c = lax.linalg.triangular_solve(a_mat, v_beta, left_side=True, lower=True,
                                      unit_diagonal=True)
    w_c = lax.linalg.triangular_solve(a_mat, k_beta * jnp.exp(gc)[..., None], left_side=True,
                                      lower=True, unit_diagonal=True)
    xs = tuple(jnp.moveaxis(t, 2, 0) for t in (qc, kc, u_c, w_c, gc, decay))

    def step(state, inp):
        q_i, k_i, u_i, w_i, g_i, dec_i = inp
        v_new = u_i - jnp.einsum("bhcd,bhde->bhce", w_i, state)
        intra = jnp.einsum("bhcd,bhsd->bhcs", q_i, k_i) * dec_i
        o = (jnp.einsum("bhcd,bhde->bhce", q_i * jnp.exp(g_i)[..., None], state)
             + jnp.einsum("bhcs,bhse->bhce", intra, v_new))
        g_last = g_i[..., -1]
        k_dec = k_i * jnp.exp(g_last[..., None] - g_i)[..., None]
        state = state * jnp.exp(g_last)[..., None, None] + jnp.einsum("bhcd,bhce->bhde", k_dec, v_new)
        return state, o

    s0 = jnp.zeros((bsz, nh, dk, dv), jnp.float32)
    _, out = lax.scan(step, s0, xs)
    return out.transpose(1,0, 3, 2, 4).reshape(bsz, length, nh, dv)


def gated_deltanet(proj, conv_w, dt_bias, a_log, norm_w):
    bsz, length, _ = proj.shape
    qkv = jax.nn.silu(causal_depthwise_conv(proj[..., :QKV_END], conv_w)).astype(jnp.float32)
    q, k, v = jnp.split(qkv, 3, axis=-1)
    shp = (bsz, length, DN_HEADS, DN_HEAD_DIM)
    q = l2_normalize(q.reshape(shp)) * (DN_HEAD_DIM ** -0.5)
    k = l2_normalize(k.reshape(shp))
    v = v.reshape(shp)
    z = proj[..., QKV_END:Z_END].astype(jnp.float32).reshape(shp)
    beta = jax.nn.sigmoid(proj[..., Z_END:BETA_END].astype(jnp.float32))
    g = -jnp.exp(a_log.astype(jnp.float32)) * jax.nn.softplus(
        proj[..., BETA_END:ALPHA_END].astype(jnp.float32) + dt_bias.astype(jnp.float32))
    o = chunk_gated_delta_rule(q, k, v, g, beta)
    o = o * lax.rsqrt(jnp.mean(o * o, axis=-1, keepdims=True) + NORM_EPS)
    o = o * norm_w.astype(jnp.float32) * jax.nn.silu(z)
    return o.reshape(bsz, length, DN_WIDTH)


def _complex_affine_combine(e1, e2):
    a1r, a1i, b1r, b1i = e1
    a2r, a2i, b2r, b2i = e2
    return (a2r * a1r - a2i * a1i, a2r * a1i + a2i * a1r,
            a2r * b1r - a2i * b1i + b2r, a2r * b1i + a2i * b1r + b2i)


def s5_ssm(u, a_re, a_im, log_dt, b_re, b_im, c_re, c_im, d, glu_w, glu_b, norm_w):
    bsz, length, _ = u.shape
    f32 = jnp.float32
    uf = u.astype(f32)
    dt = jnp.exp(log_dt.astype(f32))[:, None]
    ar, ai = a_re.astype(f32), a_im.astype(f32)
    mag = jnp.exp(ar * dt)
    lb_re, lb_im = mag * jnp.cos(ai * dt), mag * jnp.sin(ai * dt)
    den = ar * ar + ai * ai
    nr, ni = lb_re - 1.0, lb_im
    f_re = (nr * ar + ni * ai) / den
    f_im = (ni * ar - nr * ai) / den
    br, bi = b_re.astype(f32), b_im.astype(f32)
    bb_re = f_re[..., None] * br - f_im[..., None] * bi
    bb_im = f_re[..., None] * bi + f_im[..., None] * br
    cr, ci = c_re.astype(f32), c_im.astype(f32)
    dg = d.astype(f32).reshape(SSM_GROUPS, SSM_GROUP)
    n = length // SSM_CHUNK
    u_chunks = uf.reshape(bsz, n, SSM_CHUNK, SSM_GROUPS, SSM_GROUP).transpose(1, 0, 2, 3, 4)

    def step(carry, u_blk):
        h_re, h_im = carry
        bu_re = jnp.einsum("gnp,bcgp->bcgn", bb_re, u_blk)
        bu_im = jnp.einsum("gnp,bcgp->bcgn", bb_im, u_blk)
        elems = (jnp.broadcast_to(lb_re, bu_re.shape), jnp.broadcast_to(lb_im, bu_im.shape),
                 bu_re, bu_im)
        p_re, p_im, s_re, s_im = lax.associative_scan(_complex_affine_combine, elems, axis=1)
        st_re = s_re + p_re * h_re[:, None] - p_im * h_im[:, None]
        st_im = s_im + p_re * h_im[:, None] + p_im * h_re[:, None]
        y = (jnp.einsum("gpn,bcgn->bcgp", cr, st_re) - jnp.einsum("gpn,bcgn->bcgp", ci, st_im)
             + dg * u_blk)
        return (st_re[:, -1], st_im[:, -1]), y

    h0 = jnp.zeros((bsz, SSM_GROUPS, SSM_STATE), f32)
    _, y = lax.scan(step, (h0, h0), u_chunks)
    y = y.transpose(1, 0, 2, 3, 4).reshape(bsz, length, SSM_WIDTH)
    y = jax.nn.gelu(y, approximate=False)
    y = y * jax.nn.sigmoid(y @ glu_w.astype(f32) + glu_b.astype(f32))
    return rms_norm(y, norm_w)


def peer_ffn(xn, w_query, sub_keys, expert_u, expert_v):
    bsz, length, dm = xn.shape
    t = bsz * length
    xt = xn.reshape(t, dm)
    q = (xt @ w_query).astype(jnp.float32).reshape(t, PEER_HEADS, 2, PEER_HALF)
    s = jnp.einsum("thcd,ckd->thck", q, sub_keys.astype(jnp.float32))
    s_top, i_top = lax.top_k(s, PEER_TOPK)
    cand = s_top[..., 0, :, None] + s_top[..., 1, None, :]
    best, pos = lax.top_k(cand.reshape(t, PEER_HEADS, PEER_TOPK * PEER_TOPK), PEER_TOPK)
    i1 = jnp.take_along_axis(i_top[..., 0, :], pos // PEER_TOPK, axis=-1)
    i2 = jnp.take_along_axis(i_top[..., 1, :], pos % PEER_TOPK, axis=-1)
    expert = i1 * PEER_N_KEYS + i2
    gate = jax.nn.softmax(best, axis=-1)
    nb = t // PEER_TOKEN_BLOCK

    def block(args):
        xb, eb, gb = args
        act = jax.nn.gelu(jnp.einsum("td,thkd->thk", xb, expert_u[eb]).astype(jnp.float32),
                          approximate=False)
        return jnp.einsum("thk,thkd->td", (gb * act).astype(xb.dtype), expert_v[eb])

    out = lax.map(block, (xt.reshape(nb, PEER_TOKEN_BLOCK, dm),
                          expert.reshape(nb, PEER_TOKEN_BLOCK, PEER_HEADS, PEER_TOPK),
                          gate.reshape(nb, PEER_TOKEN_BLOCK, PEER_HEADS, PEER_TOPK)))
    return out.reshape(bsz, length, dm).astype(xn.dtype)


def setup_inputs(seed: int = 0) -> dict:
    key = jax.random.key(seed)
    ks = jax.random.split(key, 28)
    f32 = jnp.float32
    nl = DEPTH

    def nrm(k, shape, scale):
        return scale * jax.random.normal(k, shape, f32)

    dt = jnp.exp(jax.random.uniform(ks[4], (nl, DN_HEADS), f32, math.log(1e-3), math.log(1e-1)))
    n_idx = jnp.arange(SSM_STATE, dtype=f32)
    gns = (nl, SSM_GROUPS, SSM_STATE)
    return {
        "x": nrm(ks[0], (BATCH, SEQ, D_MODEL), 1.0),
        "attn_norm_w": 1.0 + nrm(ks[1], (nl, D_MODEL), 0.02),
        "w_in": nrm(ks[2], (nl, D_MODEL, IN_COLS), D_MODEL ** -0.5),
        "conv_w": nrm(ks[3], (nl, CONV_WIDTH, 3 * DN_WIDTH), CONV_WIDTH ** -0.5),
        "dn_dt_bias": dt + jnp.log(-jnp.expm1(-dt)),
        "dn_a_log": jnp.log(jax.random.uniform(ks[5], (nl, DN_HEADS), f32, 1.0, 16.0)),
        "dn_norm_w": 1.0 + nrm(ks[6], (nl, DN_HEAD_DIM), 0.02),
        "ssm_a_re": -0.5 + nrm(ks[7], gns, 0.01),
        "ssm_a_im": math.pi * n_idx + nrm(ks[8], gns, 0.01),
        "ssm_log_dt": jax.random.uniform(ks[9], (nl, SSM_GROUPS), f32, math.log(1e-3), math.log(1e-1)),
        "ssm_b_re": nrm(ks[10], (nl, SSM_GROUPS, SSM_STATE, SSM_GROUP), (2 * SSM_GROUP) ** -0.5),
        "ssm_b_im": nrm(ks[11], (nl, SSM_GROUPS, SSM_STATE, SSM_GROUP), (2 * SSM_GROUP) ** -0.5),
        "ssm_c_re": nrm(ks[12], (nl, SSM_GROUPS, SSM_GROUP, SSM_STATE), SSM_STATE ** -0.5),
        "ssm_c_im": nrm(ks[13], (nl, SSM_GROUPS, SSM_GROUP, SSM_STATE), SSM_STATE ** -0.5),
        "ssm_d": nrm(ks[14], (nl, SSM_WIDTH), 1.0),
        "ssm_glu_w": nrm(ks[15], (nl, SSM_WIDTH, SSM_WIDTH), SSM_WIDTH ** -0.5),
        "ssm_glu_b": nrm(ks[16], (nl, SSM_WIDTH), 0.01),
        "ssm_norm_w": 1.0 + nrm(ks[17], (nl, SSM_WIDTH), 0.02),
        "w_out": nrm(ks[18], (nl, MIX_WIDTH, D_MODEL), MIX_WIDTH ** -0.5),
        "ffn_norm_w": 1.0 + nrm(ks[19], (nl, D_MODEL), 0.02),
        "peer_w_query": nrm(ks[20], (nl, D_MODEL, PEER_HEADS * PEER_QUERY_DIM), D_MODEL ** -0.5),
        "peer_sub_keys": nrm(ks[21], (nl, 2, PEER_N_KEYS, PEER_HALF), PEER_HALF ** -0.5),
        "peer_u": nrm(ks[22], (nl, PEER_EXPERTS, D_MODEL), D_MODEL ** -0.5),
        "peer_v": nrm(ks[23], (nl, PEER_EXPERTS, D_MODEL), PEER_HEADS ** -0.5),
        "final_norm_w": 1.0 + nrm(ks[24], (D_MODEL,), 0.02),
    }


def reference(x, attn_norm_w, w_in, conv_w, dn_dt_bias, dn_a_log, dn_norm_w,
              ssm_a_re, ssm_a_im, ssm_log_dt, ssm_b_re, ssm_b_im, ssm_c_re, ssm_c_im,
              ssm_d, ssm_glu_w, ssm_glu_b, ssm_norm_w, w_out, ffn_norm_w,
              peer_w_query, peer_sub_keys, peer_u, peer_v, final_norm_w):
    h = x
    for layer in range(DEPTH):
        xn = rms_norm(h, attn_norm_w[layer])
        proj = xn @ w_in[layer]
        mix_a = gated_deltanet(proj[..., :ALPHA_END], conv_w[layer], dn_dt_bias[layer],
                               dn_a_log[layer], dn_norm_w[layer])
        mix_b = s5_ssm(proj[..., ALPHA_END:], ssm_a_re[layer], ssm_a_im[layer], ssm_log_dt[layer],
                       ssm_b_re[layer], ssm_b_im[layer], ssm_c_re[layer], ssm_c_im[layer],
                       ssm_d[layer], ssm_glu_w[layer], ssm_glu_b[layer], ssm_norm_w[layer])
        mixed = jnp.concatenate([mix_a.astype(h.dtype), mix_b.astype(h.dtype)], axis=-1)
        h = h + mixed @ w_out[layer]
        h = h + peer_ffn(rms_norm(h, ffn_norm_w[layer]), peer_w_query[layer],
                         peer_sub_keys[layer], peer_u[layer], peer_v[layer])
    return rms_norm(h, final_norm_w)
```

```python
import functools
import math

import jax
import jax.numpy as jnp
from jax import lax
from jax.experimental import pallas as pl
from jax.experimental.pallas import tpu as pltpu

f32 = jnp.float32
bf16 = jnp.bfloat16

NORM_EPS = 1e-6
CONV_WIDTH = 4
PEER_TOPK = 16
LANES = 128
SUBLANES = 8
VMEM_LIMIT_BYTES = 56 * 1024 * 1024


def _params(semantics, vmem=None):
    return pltpu.CompilerParams(dimension_semantics=semantics, vmem_limit_bytes=vmem)


def _pick_tile(n, target, quantum=LANES):
    if n <= target:
        return n
    best = None
    for t in range(quantum, target + 1, quantum):
        if n % t == 0:
            best = t
    assert best is not None, (n, target, quantum)
    return best


def _dot1(a, b):
    return jnp.dot(a.astype(bf16), b.astype(bf16), preferred_element_type=f32)


def _split(x):
    hi = x.astype(bf16)
    lo = (x - hi.astype(f32)).astype(bf16)
    return hi, lo


def _dot3(a, b, dims=(((1,), (0,)), ((), ()))):
    ah, al = _split(a)
    bh, bl = _split(b)
    dg = functools.partial(lax.dot_general, dimension_numbers=dims, preferred_element_type=f32)
    return dg(ah, bh) + (dg(ah, bl) + dg(al, bh))


def _gelu(x):
    return 0.5 * x * (1.0 + lax.erf(x * (1.0 / math.sqrt(2.0))))


def _silu(x):
    return x * jax.nn.sigmoid(x)


def _rmsnorm_kernel(x_ref, w_ref, o_ref):
    x = x_ref[...]
    ms = jnp.mean(x * x, axis=-1, keepdims=True)
    o_ref[...] = (x * lax.rsqrt(ms + NORM_EPS) * w_ref[...]).astype(o_ref.dtype)


def _rmsnorm(x2d, w, out_dtype, tm):
    t, d = x2d.shape
    return pl.pallas_call(
        _rmsnorm_kernel,
        grid=(t // tm,),
        in_specs=[pl.BlockSpec((tm, d), lambda i: (i, 0)), pl.BlockSpec((1, d), lambda i: (0, 0))],
        out_specs=pl.BlockSpec((tm, d), lambda i: (i, 0)),
        out_shape=jax.ShapeDtypeStruct((t, d), out_dtype),
        compiler_params=_params(("parallel",), VMEM_LIMIT_BYTES),
        name="rmsnorm",
    )(x2d, w.reshape(1, d).astype(f32))


def _matmul_kernel(a_ref, b_ref, o_ref):
    o_ref[...] = jnp.dot(a_ref[...], b_ref[...], preferred_element_type=f32).astype(o_ref.dtype)


def _matmul(a, b, tm, tn, name):
    m, k = a.shape
    _, n = b.shape
    return pl.pallas_call(
        _matmul_kernel,
        grid=(m // tm, n // tn),
        in_specs=[pl.BlockSpec((tm, k), lambda i, j: (i, 0)), pl.BlockSpec((k, tn), lambda i, j: (0, j))],
        out_specs=pl.BlockSpec((tm, tn), lambda i, j: (i, j)),
        out_shape=jax.ShapeDtypeStruct((m, n), f32),
        compiler_params=_params(("parallel", "parallel"), VMEM_LIMIT_BYTES),
        name=name,
    )(a, b)


def _gates_kernel(ba_ref, alog_ref, dtb_ref, o_ref, *, heads):
    x = ba_ref[...].T
    beta = jax.nn.sigmoid(x[0:heads])
    a = x[heads:2 * heads] + dtb_ref[...]
    softplus = jnp.maximum(a, 0.0) + jnp.log1p(jnp.exp(-jnp.abs(a)))
    o_ref[0:heads, :] = beta
    o_ref[heads:2 * heads, :] = -jnp.exp(alog_ref[...]) * softplus


def _dn_gates(proj, ba_col_block, a_log, dt_bias, batch, length, tr):
    heads = a_log.shape[0]
    nrb = length // tr
    return pl.pallas_call(
        functools.partial(_gates_kernel, heads=heads),
        grid=(batch, nrb),
        in_specs=[
            pl.BlockSpec((tr, LANES), lambda b, r: (b * nrb + r, ba_col_block)),
            pl.BlockSpec((heads, 1), lambda b, r: (0, 0)),
            pl.BlockSpec((heads, 1), lambda b, r: (0, 0)),
        ],
        out_specs=pl.BlockSpec((None, 2 * heads, tr), lambda b, r: (b, 0, r)),
        out_shape=jax.ShapeDtypeStruct((batch, 2 * heads, length), f32),
        compiler_params=_params(("parallel", "parallel")),
        name="dn_gates",
    )(proj, a_log.reshape(heads, 1).astype(f32), dt_bias.reshape(heads, 1).astype(f32))


def _unit_lower_inverse(a, row, col):
    c = a.shape[0]
    eye = (row == col).astype(f32)
    base = 16
    blk = lambda idx, size: jnp.right_shift(idx, size.bit_length() - 1)
    ld = jnp.where(blk(row, base) == blk(col, base), a, 0.0)
    l2 = _dot3(ld, ld)
    l4 = _dot3(l2, l2)
    l8 = _dot3(l4, l4)
    x = eye - ld
    x = x + _dot3(x, l2)
    x = x + _dot3(x, l4)
    x = x + _dot3(x, l8)
    n = base
    while n < c:
        a_off = jnp.where(blk(row, 2 * n) == blk(col, 2 * n),
                          jnp.where(blk(row, n) != blk(col, n), a, 0.0), 0.0)
        x = x - _dot3(_dot3(x, a_off), x)
        n *= 2
    return x


def _dn_prep_kernel(q_ref, k_ref, v_ref, qh_ref, kh_ref, vh_ref, wq_ref, wk_ref, wv_ref,
                    beta_ref, g_ref, qp_ref, op_ref, wm_ref, nm_ref, dg_ref, *, cpb):
    c = LANES
    rows = cpb * c
    first = pl.program_id(2) == 0

    def conv_silu(x_ref, h_ref, w_ref):
        x = x_ref[...]
        halo = jnp.where(first, 0.0, h_ref[...])
        xx = jnp.concatenate([halo, x], axis=0)
        w = w_ref[...]
        y = w[3:4] * x
        for d in range(1, CONV_WIDTH):
            y = y + w[3 - d:4 - d] * xx[SUBLANES - d:SUBLANES - d + rows]
        return _silu(y)

    def l2n(x):
        return x * lax.rsqrt(jnp.sum(x * x, axis=-1, keepdims=True) + NORM_EPS)

    q_all = l2n(conv_silu(q_ref, qh_ref, wq_ref)) * (c ** -0.5)
    k_all = l2n(conv_silu(k_ref, kh_ref, wk_ref))
    v_all = conv_silu(v_ref, vh_ref, wv_ref)

    row = lax.broadcasted_iota(jnp.int32, (c, c), 0)
    col = lax.broadcasted_iota(jnp.int32, (c, c), 1)
    causal = row >= col
    strict = row > col
    tril01 = causal.astype(bf16)

    for ci in range(cpb):
        sl = slice(ci * c, (ci + 1) * c)
        qc, kc, vc = q_all[sl], k_all[sl], v_all[sl]
        beta_col = jnp.broadcast_to(beta_ref[:, sl], (c, c)).T
        g_col = jnp.broadcast_to(g_ref[:, sl], (c, c)).T
        g_hi = g_col.astype(bf16)
        r1 = g_col - g_hi.astype(f32)
        g_mid = r1.astype(bf16)
        g_lo = (r1 - g_mid.astype(f32)).astype(bf16)
        gc_col = (jnp.dot(tril01, g_hi, preferred_element_type=f32)
                  + (jnp.dot(tril01, g_mid, preferred_element_type=f32)
                     + jnp.dot(tril01, g_lo, preferred_element_type=f32)))
        gc_row = gc_col.T
        decay = jnp.where(causal, jnp.exp(jnp.where(causal, gc_col - gc_row, 0.0)), 0.0)
        k_beta = kc * beta_col
        v_beta = vc * beta_col
        nt = (((1,), (1,)), ((), ()))
        kk = lax.dot_general(k_beta.astype(bf16), kc.astype(bf16), nt, preferred_element_type=f32)
        qk = lax.dot_general(qc.astype(bf16), kc.astype(bf16), nt, preferred_element_type=f32)
        a_mat = jnp.where(strict, kk * decay, 0.0)
        t_inv = _unit_lower_inverse(a_mat, row, col)
        eg = jnp.exp(gc_col)
        uw = _dot3(t_inv, jnp.concatenate([k_beta * eg, v_beta], axis=1))
        intra = qk * decay
        iw = _dot1(intra, uw)
        gl = gc_col[c - 1:c, :]
        k_dec = kc * jnp.exp(gl - gc_col)
        tn = (((0,), (0,)), ((), ()))
        kw = lax.dot_general(k_dec.astype(bf16), uw.astype(bf16), tn, preferred_element_type=f32)
        qp_ref[sl, :] = qc * eg - iw[:, :c]
        op_ref[sl, :] = iw[:, c:]
        wm_ref[sl, :] = kw[:, :c]
        nm_ref[sl, :] = kw[:, c:]
        dg_ref[ci * SUBLANES:(ci + 1) * SUBLANES, :] = jnp.broadcast_to(jnp.exp(gl), (SUBLANES, c))


def _dn_prep(proj, conv_w, gates4, batch, length, heads, cpb):
    c = LANES
    rows = cpb * c
    nrb = length // rows
    hb8 = rows // SUBLANES

    def xspec(off):
        return pl.BlockSpec((rows, c), lambda b, h, r: (b * nrb + r, off + h))

    def hspec(off):
        return pl.BlockSpec(
            (SUBLANES, c),
            lambda b, h, r: (jnp.maximum((b * nrb + r) * hb8 - 1, 0), off + h))

    def wspec(off):
        return pl.BlockSpec((CONV_WIDTH, c), lambda b, h, r: (0, off + h))

    def gspec(off):
        return pl.BlockSpec((None, None, 1, rows), lambda b, h, r: (b, off + h, 0, r))

    big = jax.ShapeDtypeStruct((batch, heads, length, c), f32)
    ospec = pl.BlockSpec((None, None, rows, c), lambda b, h, r: (b, h, r, 0))
    dg_shape = jax.ShapeDtypeStruct((batch, heads, length // c * SUBLANES, c), f32)
    dg_spec = pl.BlockSpec((None, None, cpb * SUBLANES, c), lambda b, h, r: (b, h, r, 0))
    return pl.pallas_call(
        functools.partial(_dn_prep_kernel, cpb=cpb),
        grid=(batch, heads, nrb),
        in_specs=[xspec(0), xspec(heads), xspec(2 * heads),
                  hspec(0), hspec(heads), hspec(2 * heads),
                  wspec(0), wspec(heads), wspec(2 * heads),
                  gspec(0), gspec(heads)],
        out_specs=[ospec, ospec, ospec, ospec, dg_spec],
        out_shape=[big, big, big, big, dg_shape],
        compiler_params=_params(("parallel", "parallel", "parallel"), VMEM_LIMIT_BYTES),
        name="dn_prep",
    )(proj, proj, proj, proj, proj, proj, conv_w, conv_w, conv_w, gates4, gates4)


def _dn_seq_kernel(qp_ref, op_ref, wm_ref, nm_ref, dg_ref, z_ref, nw_ref, o_ref, s_scr, *, hb):
    c = LANES

    @pl.when(pl.program_id(1) == 0)
    def _():
        s_scr[...] = jnp.zeros_like(s_scr)

    for i in range(hb):
        s = s_scr[i]
        s_b = s.astype(bf16)
        o = jnp.dot(qp_ref[i].astype(bf16), s_b, preferred_element_type=f32) + op_ref[i]
        s_scr[i] = (dg_ref[i][0:1, :] * s
                    - jnp.dot(wm_ref[i].astype(bf16), s_b, preferred_element_type=f32) + nm_ref[i])
        o = o * lax.rsqrt(jnp.mean(o * o, axis=-1, keepdims=True) + NORM_EPS)
        o = o * nw_ref[...] * _silu(z_ref[:, i * c:(i + 1) * c])
        o_ref[:, i * c:(i + 1) * c] = o.astype(o_ref.dtype)


def _dn_seq(qp, op, wm, nm, dg, proj, norm_w, batch, length, heads, hb, z_col_block):
    c = LANES
    nc = length // c
    hpb = heads // hb
    bh = batch * heads

    def cspec(rows):
        return pl.BlockSpec((hb, rows, c), lambda g, t: (g, t, 0))

    flat = lambda a: a.reshape(bh, a.shape[2], c)
    return pl.pallas_call(
        functools.partial(_dn_seq_kernel, hb=hb),
        grid=(bh // hb, nc),
        in_specs=[cspec(c), cspec(c), cspec(c), cspec(c), cspec(SUBLANES),
                  pl.BlockSpec((c, hb * c), lambda g, t: ((g // hpb) * nc + t, z_col_block + g % hpb)),
                  pl.BlockSpec((1, c), lambda g, t: (0, 0))],
        out_specs=pl.BlockSpec((c, hb * c), lambda g, t: ((g // hpb) * nc + t, g % hpb)),
        out_shape=jax.ShapeDtypeStruct((batch * length, heads * c), bf16),
        scratch_shapes=[pltpu.VMEM((hb, c, c), f32)],
        compiler_params=_params(("parallel", "arbitrary"), VMEM_LIMIT_BYTES),
        name="dn_seq",
    )(flat(qp), flat(op), flat(wm), flat(nm), flat(dg), proj, norm_w.reshape(1, c).astype(f32))


def _s5_disc_kernel(ar_ref, ai_ref, ldt_ref, br_ref, bi_ref, lr_ref, li_ref, bbr_ref, bbi_ref):
    ar, ai = ar_ref[...], ai_ref[...]
    dt = jnp.exp(ldt_ref[...])
    mag = jnp.exp(ar * dt)
    lb_re, lb_im = mag * jnp.cos(ai * dt), mag * jnp.sin(ai * dt)
    den = ar * ar + ai * ai
    nr, ni = lb_re - 1.0, lb_im
    f_re = (nr * ar + ni * ai) / den
    f_im = (ni * ar - nr * ai) / den
    lr_ref[...] = lb_re
    li_ref[...] = lb_im
    br, bi = br_ref[...], bi_ref[...]
    bbr_ref[...] = f_re * br - f_im * bi
    bbi_ref[...] = f_re * bi + f_im * br


def _s5_discretise(a_re, a_im, log_dt, b_re_t, b_im_t):
    g, n = a_re.shape
    p = b_re_t.shape[1]
    return pl.pallas_call(
        _s5_disc_kernel,
        out_shape=[jax.ShapeDtypeStruct((g, 1, n), f32), jax.ShapeDtypeStruct((g, 1, n), f32),
                   jax.ShapeDtypeStruct((g, p, n), f32), jax.ShapeDtypeStruct((g, p, n), f32)],
        name="s5_discretise",
    )(a_re.reshape(g, 1, n).astype(f32), a_im.reshape(g, 1, n).astype(f32),
      log_dt.reshape(g, 1, 1).astype(f32), b_re_t.astype(f32), b_im_t.astype(f32))


def _s5_scan_kernel(u_ref, bm_ref, cre_ref, cim_ref, lr_ref, li_ref, d_ref, o_ref,
                    bu_scr, hs_scr, st_scr, *, nb, tt, sh):
    nq = 2 * sh // LANES

    @pl.when(pl.program_id(2) == 0)
    def _():
        st_scr[...] = jnp.zeros_like(st_scr)

    u = u_ref[...]
    for j in range(nb):
        bu = _dot1(u[:, j * LANES:(j + 1) * LANES], bm_ref[j])
        for q in range(nq):
            bu_scr[q, j * tt:(j + 1) * tt, :] = bu[:, q * LANES:(q + 1) * LANES]
    lr, li = lr_ref[...], li_ref[...]

    def body(t, carry):
        hr, hi = carry
        bu = jnp.concatenate([bu_scr[q, pl.ds(t, nb, stride=tt), :] for q in range(nq)], axis=1)
        nr = lr * hr - li * hi + bu[:, :sh]
        ni = lr * hi + li * hr + bu[:, sh:]
        for q in range(nq // 2):
            hs_scr[q, pl.ds(t, nb, stride=tt), :] = nr[:, q * LANES:(q + 1) * LANES]
            hs_scr[nq // 2 + q, pl.ds(t, nb, stride=tt), :] = ni[:, q * LANES:(q + 1) * LANES]
        return nr, ni

    hr, hi = lax.fori_loop(0, tt, body, (st_scr[0], st_scr[1]))
    st_scr[0] = hr
    st_scr[1] = hi
    for j in range(nb):
        h_re = jnp.concatenate([hs_scr[q, j * tt:(j + 1) * tt, :] for q in range(nq // 2)], axis=1)
        h_im = jnp.concatenate([hs_scr[nq // 2 + q, j * tt:(j + 1) * tt, :] for q in range(nq // 2)], axis=1)
        y = (_dot1(h_re, cre_ref[j]) - _dot1(h_im, cim_ref[j])
             + d_ref[:, j * LANES:(j + 1) * LANES] * u[:, j * LANES:(j + 1) * LANES])
        o_ref[:, j * LANES:(j + 1) * LANES] = _gelu(y)


def _s5_scan(proj, u_col_block, bm, cre, cim, lr, li, d, batch, length, nb, tt):
    nblk, _, sh2 = bm.shape
    sh = sh2 // 2
    nh = nblk // nb
    nt = length // tt
    width = nb * LANES
    return pl.pallas_call(
        functools.partial(_s5_scan_kernel, nb=nb, tt=tt, sh=sh),
        grid=(batch, nh, nt),
        in_specs=[
            pl.BlockSpec((tt, width), lambda b, h, t: (b * nt + t, u_col_block + h)),
            pl.BlockSpec((nb, LANES, sh2), lambda b, h, t: (h, 0, 0)),
            pl.BlockSpec((nb, sh, LANES), lambda b, h, t: (h, 0, 0)),
            pl.BlockSpec((nb, sh, LANES), lambda b, h, t: (h, 0, 0)),
            pl.BlockSpec((nb, sh), lambda b, h, t: (h, 0)),
            pl.BlockSpec((nb, sh), lambda b, h, t: (h, 0)),
            pl.BlockSpec((1, width), lambda b, h, t: (0, h)),
        ],
        out_specs=pl.BlockSpec((tt, width), lambda b, h, t: (b * nt + t, h)),
        out_shape=jax.ShapeDtypeStruct((batch * length, nblk * LANES), f32),
        scratch_shapes=[pltpu.VMEM((sh2 // LANES, nb * tt, LANES), f32),
                        pltpu.VMEM((sh2 // LANES, nb * tt, LANES), f32),
                        pltpu.VMEM((2, nb, sh), f32)],
        compiler_params=_params(("parallel", "parallel", "arbitrary"), VMEM_LIMIT_BYTES),
        name="s5_scan",
    )(proj, bm, cre, cim, lr, li, d.reshape(1, -1).astype(f32))


def _glu_norm_kernel(y_ref, w_ref, b_ref, nw_ref, o_ref):
    y = y_ref[...]
    gate = jnp.dot(y.astype(bf16), w_ref[...], preferred_element_type=f32) + b_ref[...]
    y = y * jax.nn.sigmoid(gate)
    ms = jnp.mean(y * y, axis=-1, keepdims=True)
    o_ref[...] = (y * lax.rsqrt(ms + NORM_EPS) * nw_ref[...]).astype(o_ref.dtype)


def _glu_norm(y, w_bf, b, nw, tm):
    t, n = y.shape
    row = lambda a: a.reshape(1, n).astype(f32)
    return pl.pallas_call(
        _glu_norm_kernel,
        grid=(t // tm,),
        in_specs=[pl.BlockSpec((tm, n), lambda i: (i, 0)), pl.BlockSpec((n, n), lambda i: (0, 0)),
                  pl.BlockSpec((1, n), lambda i: (0, 0)), pl.BlockSpec((1, n), lambda i: (0, 0))],
        out_specs=pl.BlockSpec((tm, n), lambda i: (i, 0)),
        out_shape=jax.ShapeDtypeStruct((t, n), bf16),
        compiler_params=_params(("parallel",), VMEM_LIMIT_BYTES),
        name="s5_glu_norm",
    )(y, w_bf, row(b), row(nw))


def _out_proj_kernel(a_ref, b_ref, wa_ref, wb_ref, x_ref, o_ref):
    o_ref[...] = (x_ref[...]
                  + jnp.dot(a_ref[...], wa_ref[...], preferred_element_type=f32)
                  + jnp.dot(b_ref[...], wb_ref[...], preferred_element_type=f32))


def _out_proj(mix_a, mix_b, wa, wb, x2d, tm, tn):
    t, d = x2d.shape
    ka, kb = mix_a.shape[1], mix_b.shape[1]
    return pl.pallas_call(
        _out_proj_kernel,
        grid=(t // tm, d // tn),
        in_specs=[pl.BlockSpec((tm, ka), lambda i, j: (i, 0)), pl.BlockSpec((tm, kb), lambda i, j: (i, 0)),
                  pl.BlockSpec((ka, tn), lambda i, j: (0, j)), pl.BlockSpec((kb, tn), lambda i, j: (0, j)),
                  pl.BlockSpec((tm, tn), lambda i, j: (i, j))],
        out_specs=pl.BlockSpec((tm, tn), lambda i, j: (i, j)),
        out_shape=jax.ShapeDtypeStruct((t, d), f32),
        compiler_params=_params(("parallel", "parallel"), VMEM_LIMIT_BYTES),
        name="out_proj",
    )(mix_a, mix_b, wa, wb, x2d)


def _peer_route_kernel(q_ref, keys_ref, thr_ref, e1_ref, s2_ref, e2_ref, top_scr, cand_scr, *, n_cand):
    neg = -jnp.inf
    k = PEER_TOPK
    nt = (((1,), (1,)), ((), ()))
    scores = []
    for half in range(2):
        s = _dot3(keys_ref[half], q_ref[:, half * LANES:(half + 1) * LANES], nt)
        scores.append(s)
        work = s
        for i in range(k):
            m = jnp.max(work, axis=0, keepdims=True)
            top_scr[half, i:i + 1, :] = m
            work = jnp.where(work >= m, neg, work)
    top1, top2 = top_scr[0], top_scr[1]
    cand_scr[...] = jnp.full(cand_scr.shape, neg, f32)
    off = 0
    for i in range(k):
        n_i = k // (i + 1)
        cand_scr[off:off + n_i, :] = top1[i:i + 1, :] + top2[0:n_i, :]
        off += n_i
    assert off == n_cand
    cand = cand_scr[...]
    work = cand
    best = None
    m = None
    for i in range(k):
        m = jnp.max(work, axis=0, keepdims=True)
        if i == 0:
            best = m
        work = jnp.where(work >= m, neg, work)
    theta = m
    z = jnp.sum(jnp.where(cand >= theta, jnp.exp(cand - best), 0.0), axis=0, keepdims=True)
    thr_ref[...] = theta - scores[0]
    e1_ref[...] = jnp.exp(scores[0] - top1[0:1, :]) / z
    s2_ref[...] = scores[1]
    e2_ref[...] = jnp.exp(scores[1] - top2[0:1, :])


def _peer_route(q, sub_keys, heads, tm):
    t = q.shape[0]
    nk, half = sub_keys.shape[1], sub_keys.shape[2]
    assert nk == LANES and half == LANES
    n_cand = sum(PEER_TOPK // (i + 1) for i in range(PEER_TOPK))
    cand_rows = -(-n_cand // SUBLANES) * SUBLANES
    out = jax.ShapeDtypeStruct((heads, nk, t), f32)
    ospec = pl.BlockSpec((None, nk, tm), lambda i, h: (h, 0, i))
    return pl.pallas_call(
        functools.partial(_peer_route_kernel, n_cand=n_cand),
        grid=(t // tm, heads),
        in_specs=[pl.BlockSpec((tm, 2 * half), lambda i, h: (i, h)),
                  pl.BlockSpec((2, nk, half), lambda i, h: (0, 0, 0))],
        out_specs=[ospec, ospec, ospec, ospec],
        out_shape=[out, out, out, out],
        scratch_shapes=[pltpu.VMEM((2, PEER_TOPK, tm), f32), pltpu.VMEM((cand_rows, tm), f32)],
        compiler_params=_params(("parallel", "parallel"), VMEM_LIMIT_BYTES),
        name="peer_route",
    )(q, sub_keys.astype(f32))


def _peer_dense_kernel(x_ref, u_ref, vt_ref, thr_ref, e1_ref, s2_ref, e2_ref, o_ref, p_scr, *, heads, na):
    e = pl.program_id(1)
    nt = (((1,), (1,)), ((), ()))
    act = lax.dot_general(u_ref[...], x_ref[...], nt, preferred_element_type=f32)
    for al in range(na):
        a = e * na + al
        gate = None
        for h in range(heads):
            thr = thr_ref[h, pl.ds(a, 1), :]
            e1 = e1_ref[h, pl.ds(a, 1), :]
            term = jnp.where(s2_ref[h] >= thr, e2_ref[h], 0.0) * e1
            gate = term if gate is None else gate + term
        sl = slice(al * LANES, (al + 1) * LANES)
        p_scr[sl, :] = (gate * _gelu(act[sl, :])).astype(bf16)
    @pl.when(e == 0)
    def _():
        o_ref[...] = jnp.zeros_like(o_ref)

    o_ref[...] += jnp.dot(vt_ref[...], p_scr[...], preferred_element_type=f32)


def _peer_dense(xn, u_bf, vt_bf, thr, e1, s2, e2, tm, te):
    t, d = xn.shape
    ne = u_bf.shape[0]
    heads, nk, _ = thr.shape
    na = te // nk
    rspec = pl.BlockSpec((heads, nk, tm), lambda i, e: (0, 0, i))
    return pl.pallas_call(
        functools.partial(_peer_dense_kernel, heads=heads, na=na),
        grid=(t // tm, ne // te),
        in_specs=[pl.BlockSpec((tm, d), lambda i, e: (i, 0)),
                  pl.BlockSpec((te, d), lambda i, e: (e, 0)),
                  pl.BlockSpec((d, te), lambda i, e: (0, e)),
                  rspec, rspec, rspec, rspec],
        out_specs=pl.BlockSpec((d, tm), lambda i, e: (0, i)),
        out_shape=jax.ShapeDtypeStruct((d, t), f32),
        scratch_shapes=[pltpu.VMEM((te, tm), bf16)],
        compiler_params=_params(("parallel", "arbitrary"), VMEM_LIMIT_BYTES),
        name="peer_dense",
    )(xn, u_bf, vt_bf, thr, e1, s2, e2)


def _final_kernel(h_ref, pt_ref, w_ref, o_ref):
    x = h_ref[...] + pt_ref[...].T
    ms = jnp.mean(x * x, axis=-1, keepdims=True)
    o_ref[...] = x * lax.rsqrt(ms + NORM_EPS) * w_ref[...]


def _final(h, peer_t, w, tm):
    t, d = h.shape
    return pl.pallas_call(
        _final_kernel,
        grid=(t // tm,),
        in_specs=[pl.BlockSpec((tm, d), lambda i: (i, 0)), pl.BlockSpec((d, tm), lambda i: (0, i)),
                  pl.BlockSpec((1, d), lambda i: (0, 0))],
        out_specs=pl.BlockSpec((tm, d), lambda i: (i, 0)),
        out_shape=jax.ShapeDtypeStruct((t, d), f32),
        compiler_params=_params(("parallel",), VMEM_LIMIT_BYTES),
        name="final_norm",
    )(h, peer_t, w.reshape(1, d).astype(f32))


def _block_diag(w, groups_per_block):
    g, a, b = w.shape
    nblk = g // groups_per_block
    w = w.reshape(nblk, groups_per_block, a, b)
    eye = jnp.eye(groups_per_block, dtype=w.dtype)
    return jnp.einsum("kgab,gh->kgahb", w, eye).reshape(nblk, groups_per_block * a, groups_per_block * b)


def _layer(h2d, batch, length, attn_norm_w, w_in, conv_w, dt_bias, a_log, dn_norm_w,
           a_re, a_im, log_dt, b_re, b_im, c_re, c_im, ssm_d, glu_w, glu_b, ssm_norm_w,
           w_out, ffn_norm_w, w_query, sub_keys, peer_u, peer_v, final_w):
    t, d = h2d.shape
    heads = dt_bias.shape[0]
    c = LANES
    dnw = heads * c
    ssmw = ssm_d.shape[0]
    groups, nstate = a_re.shape
    gsize = b_re.shape[-1]
    gpb = c // gsize
    assert dn_norm_w.shape[0] == c and 2 * heads <= c and groups * gsize == ssmw

    z_end = 4 * dnw
    w_ba = jnp.pad(w_in[:, z_end:z_end + 2 * heads], ((0, 0), (0, c - 2 * heads)))
    w_cat = jnp.concatenate([w_in[:, :z_end], w_in[:, z_end + 2 * heads:], w_ba], axis=1).astype(bf16)
    ncols = w_cat.shape[1]
    xn = _rmsnorm(h2d, attn_norm_w, bf16, _pick_tile(t, 256, SUBLANES))
    proj = _matmul(xn, w_cat, _pick_tile(t, 1024, SUBLANES), _pick_tile(ncols, 1152), "in_proj")

    gates = _dn_gates(proj, (z_end + ssmw) // c, a_log, dt_bias, batch, length, _pick_tile(length, 512))
    gates4 = gates.reshape(batch, 2 * heads, 1, length)
    cpb = _pick_tile(length // c, 4, 1)
    qp, op, wm, nm, dg = _dn_prep(proj, conv_w.astype(f32), gates4, batch, length, heads, cpb)
    hb = _pick_tile(heads, 8, 1)
    mix_a = _dn_seq(qp, op, wm, nm, dg, proj, dn_norm_w, batch, length, heads, hb, 3 * heads // hb)

    lr, li, bbr, bbi = _s5_discretise(a_re, a_im, log_dt,
                                      jnp.swapaxes(b_re, 1, 2), jnp.swapaxes(b_im, 1, 2))
    nblk = groups // gpb
    sh = gpb * nstate
    bm = jnp.concatenate([_block_diag(bbr, gpb), _block_diag(bbi, gpb)], axis=2).astype(bf16)
    cre = _block_diag(jnp.swapaxes(c_re, 1, 2).astype(f32), gpb).astype(bf16)
    cim = _block_diag(jnp.swapaxes(c_im, 1, 2).astype(f32), gpb).astype(bf16)
    nb = _pick_tile(nblk, SUBLANES, 1)
    y = _s5_scan(proj, z_end // (nb * c), bm, cre, cim, lr.reshape(nblk, sh), li.reshape(nblk, sh),
                 ssm_d, batch, length, nb, _pick_tile(length, 256, SUBLANES))
    mix_b = _glu_norm(y, glu_w.astype(bf16), glu_b, ssm_norm_w, _pick_tile(t, 512, SUBLANES))

    w_out_bf = w_out.astype(bf16)
    h1 = _out_proj(mix_a, mix_b, w_out_bf[:dnw], w_out_bf[dnw:], h2d,
                   _pick_tile(t, 512, SUBLANES), _pick_tile(d, 1024))

    xn2 = _rmsnorm(h1, ffn_norm_w, bf16, _pick_tile(t, 256, SUBLANES))
    qd = w_query.shape[1]
    q = _matmul(xn2, w_query.astype(bf16), _pick_tile(t, 1024, SUBLANES), _pick_tile(qd, 1024), "peer_query")
    pheads = qd // (2 * sub_keys.shape[-1])
    tm = _pick_tile(t, 512)
    thr, e1, s2, e2 = _peer_route(q, sub_keys, pheads, tm)
    peer_t = _peer_dense(xn2, peer_u.astype(bf16), peer_v.T.astype(bf16), thr, e1, s2, e2,
                         tm, _pick_tile(peer_u.shape[0], 256))
    return _final(h1, peer_t, final_w, _pick_tile(t, 256))


def kernel(x, attn_norm_w, w_in, conv_w, dn_dt_bias, dn_a_log, dn_norm_w, ssm_a_re, ssm_a_im, ssm_log_dt, ssm_b_re, ssm_b_im, ssm_c_re, ssm_c_im, ssm_d, ssm_glu_w, ssm_glu_b, ssm_norm_w, w_out, ffn_norm_w, peer_w_query, peer_sub_keys, peer_u, peer_v, final_norm_w):
    batch, length, d = x.shape
    assert attn_norm_w.shape[0] == 1, "one layer; the final norm is fused after its PEER residual"
    out = _layer(x.reshape(batch * length, d), batch, length,
                 attn_norm_w[0], w_in[0], conv_w[0], dn_dt_bias[0], dn_a_log[0], dn_norm_w[0],
                 ssm_a_re[0], ssm_a_im[0], ssm_log_dt[0], ssm_b_re[0], ssm_b_im[0], ssm_c_re[0],
                 ssm_c_im[0], ssm_d[0], ssm_glu_w[0], ssm_glu_b[0], ssm_norm_w[0], w_out[0],
                 ffn_norm_w[0], peer_w_query[0], peer_sub_keys[0], peer_u[0], peer_v[0], final_norm_w)
    return out.reshape(batch, length, d)
```

```python
import functools
import math

import jax
import jax.numpy as jnp
from jax import lax
from jax.experimental import pallas as pl
from jax.experimental.pallas import tpu as pltpu

f32 = jnp.float32
bf16 = jnp.bfloat16

NORM_EPS = 1e-6
CONV_WIDTH = 4
PEER_TOPK = 16
LANES = 128
SUBLANES = 8
VMEM_LIMIT_BYTES = 56 * 1024 * 1024


def _params(semantics, vmem=None):
    return pltpu.CompilerParams(dimension_semantics=semantics, vmem_limit_bytes=vmem)


def _pick_tile(n, target, quantum=LANES):
    if n <= target:
        return n
    best = None
    for t in range(quantum, target + 1, quantum):
        if n % t == 0:
            best = t
    assert best is not None, (n, target, quantum)
    return best


def _dot1(a, b):
    return jnp.dot(a.astype(bf16), b.astype(bf16), preferred_element_type=f32)


def _split(x):
    hi = x.astype(bf16)
    lo = (x - hi.astype(f32)).astype(bf16)
    return hi, lo


def _dot3(a, b, dims=(((1,), (0,)), ((), ()))):
    ah, al = _split(a)
    bh, bl = _split(b)
    dg = functools.partial(lax.dot_general, dimension_numbers=dims, preferred_element_type=f32)
    return dg(ah, bh) + (dg(ah, bl) + dg(al, bh))


def _gelu(x):
    return 0.5 * x * (1.0 + lax.erf(x * (1.0 / math.sqrt(2.0))))


def _silu(x):
    return x * jax.nn.sigmoid(x)


def _rmsnorm_kernel(x_ref, w_ref, o_ref):
    x = x_ref[...]
    ms = jnp.mean(x * x, axis=-1, keepdims=True)
    o_ref[...] = (x * lax.rsqrt(ms + NORM_EPS) * w_ref[...]).astype(o_ref.dtype)


def _rmsnorm(x2d, w, out_dtype, tm):
    t, d = x2d.shape
    return pl.pallas_call(
        _rmsnorm_kernel,
        grid=(t // tm,),
        in_specs=[pl.BlockSpec((tm, d), lambda i: (i, 0)), pl.BlockSpec((1, d), lambda i: (0, 0))],
        out_specs=pl.BlockSpec((tm, d), lambda i: (i, 0)),
        out_shape=jax.ShapeDtypeStruct((t, d), out_dtype),
        compiler_params=_params(("parallel",), VMEM_LIMIT_BYTES),
        name="rmsnorm",
    )(x2d, w.reshape(1, d).astype(f32))


def _matmul_kernel(a_ref, b_ref, o_ref):
    o_ref[...] = jnp.dot(a_ref[...], b_ref[...], preferred_element_type=f32).astype(o_ref.dtype)


def _matmul(a, b, tm, tn, name):
    m, k = a.shape
    _, n = b.shape
    return pl.pallas_call(
        _matmul_kernel,
        grid=(m // tm, n // tn),
        in_specs=[pl.BlockSpec((tm, k), lambda i, j: (i, 0)), pl.BlockSpec((k, tn), lambda i, j: (0, j))],
        out_specs=pl.BlockSpec((tm, tn), lambda i, j: (i, j)),
        out_shape=jax.ShapeDtypeStruct((m, n), f32),
        compiler_params=_params(("parallel", "parallel"), VMEM_LIMIT_BYTES),
        name=name,
    )(a, b)


def _split3(x):
    hi = x.astype(bf16)
    r1 = x - hi.astype(f32)
    mid = r1.astype(bf16)
    lo = (r1 - mid.astype(f32)).astype(bf16)
    return hi, mid, lo


def _gates_kernel(ba_ref, alog_ref, dtb_ref, o_ref, *, heads):
    c = LANES
    x = ba_ref[...].T
    o_ref[0:heads, :] = jax.nn.sigmoid(x[0:heads])
    a = x[heads:2 * heads] + dtb_ref[...]
    softplus = jnp.maximum(a, 0.0) + jnp.log1p(jnp.exp(-jnp.abs(a)))
    g = -jnp.exp(alog_ref[...]) * softplus
    triu01 = (lax.broadcasted_iota(jnp.int32, (c, c), 0)
              <= lax.broadcasted_iota(jnp.int32, (c, c), 1)).astype(bf16)
    pieces = _split3(g)
    for ci in range(g.shape[1] // c):
        sl = slice(ci * c, (ci + 1) * c)
        parts = [jnp.dot(p[:, sl], triu01, preferred_element_type=f32) for p in pieces]
        o_ref[heads:2 * heads, sl] = parts[0] + (parts[1] + parts[2])


def _dn_gates(proj, ba_col_block, a_log, dt_bias, batch, length, tr):
    heads = a_log.shape[0]
    nrb = length // tr
    return pl.pallas_call(
        functools.partial(_gates_kernel, heads=heads),
        grid=(batch, nrb),
        in_specs=[
            pl.BlockSpec((tr, LANES), lambda b, r: (b * nrb + r, ba_col_block)),
            pl.BlockSpec((heads, 1), lambda b, r: (0, 0)),
            pl.BlockSpec((heads, 1), lambda b, r: (0, 0)),
        ],
        out_specs=pl.BlockSpec((None, 2 * heads, tr), lambda b, r: (b, 0, r)),
        out_shape=jax.ShapeDtypeStruct((batch, 2 * heads, length), f32),
        compiler_params=_params(("parallel", "parallel")),
        name="dn_gates",
    )(proj, a_log.reshape(heads, 1).astype(f32), dt_bias.reshape(heads, 1).astype(f32))


def _unit_lower_inverses(mats, row, col):
    c = mats[0].shape[0]
    eye = (row == col).astype(f32)
    base = 16
    blk = lambda idx, size: jnp.right_shift(idx, size.bit_length() - 1)
    mm = lambda a, b: jnp.dot(a, b, preferred_element_type=f32)
    same = blk(row, base) == blk(col, base)
    ld = [jnp.where(same, a, 0.0) for a in mats]
    ldb = [a.astype(bf16) for a in ld]
    l2 = [mm(a, a).astype(bf16) for a in ldb]
    l4 = [mm(a, a).astype(bf16) for a in l2]
    l8 = [mm(a, a).astype(bf16) for a in l4]
    x = [eye - a for a in ld]
    for power in (l2, l4, l8):
        x = [xi + mm(xi.astype(bf16), pw) for xi, pw in zip(x, power)]
    n = base
    while n < c:
        xb = [xi.astype(bf16) for xi in x]
        y = [mm(xi, jnp.where(blk(row, 2 * n) == blk(col, 2 * n),
                              jnp.where(blk(row, n) != blk(col, n), a, 0.0), 0.0).astype(bf16)).astype(bf16)
             for xi, a in zip(xb, mats)]
        x = [xi - mm(yi, xbi) for xi, yi, xbi in zip(x, y, xb)]
        n *= 2
    return x


def _dn_prep_kernel(q_ref, k_ref, v_ref, qh_ref, kh_ref, vh_ref, wq_ref, wk_ref, wv_ref,
                    beta_ref, g_ref, qp_ref, op_ref, wm_ref, nm_ref, dg_ref, *, cpb):
    c = LANES
    rows = cpb * c
    first = pl.program_id(2) == 0

    def conv_silu(x_ref, h_ref, w_ref):
        x = x_ref[...]
        halo = jnp.where(first, 0.0, h_ref[...])
        xx = jnp.concatenate([halo, x], axis=0)
        w = w_ref[...]
        y = w[3:4] * x
        for d in range(1, CONV_WIDTH):
            y = y + w[3 - d:4 - d] * xx[SUBLANES - d:SUBLANES - d + rows]
        return _silu(y)

    def l2n(x):
        return x * lax.rsqrt(jnp.sum(x * x, axis=-1, keepdims=True) + NORM_EPS)

    q_all = l2n(conv_silu(q_ref, qh_ref, wq_ref)) * (c ** -0.5)
    k_all = l2n(conv_silu(k_ref, kh_ref, wk_ref))
    v_all = conv_silu(v_ref, vh_ref, wv_ref)

    row = lax.broadcasted_iota(jnp.int32, (c, c), 0)
    col = lax.broadcasted_iota(jnp.int32, (c, c), 1)
    causal = row >= col
    strict = row > col
    nt = (((1,), (1,)), ((), ()))
    tn = (((0,), (0,)), ((), ()))
    chunks = [slice(ci * c, (ci + 1) * c) for ci in range(cpb)]

    qc = [q_all[sl] for sl in chunks]
    kc = [k_all[sl] for sl in chunks]
    kcb = [k.astype(bf16) for k in kc]
    gc_row = [jnp.broadcast_to(g_ref[:, sl], (c, c)) for sl in chunks]
    gc_col = [g.T for g in gc_row]
    beta_col = [jnp.broadcast_to(beta_ref[:, sl], (c, c)).T for sl in chunks]
    decay = [jnp.where(causal, jnp.exp(jnp.where(causal, gcol - grow, 0.0)), 0.0)
             for gcol, grow in zip(gc_col, gc_row)]
    k_beta = [k * b for k, b in zip(kc, beta_col)]
    v_beta = [v_all[sl] * b for sl, b in zip(chunks, beta_col)]
    kk = [lax.dot_general(kb.astype(bf16), k, nt, preferred_element_type=f32) for kb, k in zip(k_beta, kcb)]
    qk = [lax.dot_general(q.astype(bf16), k, nt, preferred_element_type=f32) for q, k in zip(qc, kcb)]
    t_inv = _unit_lower_inverses([jnp.where(strict, a * d, 0.0) for a, d in zip(kk, decay)], row, col)
    eg = [jnp.exp(g) for g in gc_col]
    uw = [_dot1(t, jnp.concatenate([kb * e, vb], axis=1))
          for t, kb, e, vb in zip(t_inv, k_beta, eg, v_beta)]
    uwb = [a.astype(bf16) for a in uw]
    iw = [jnp.dot((a * d).astype(bf16), b, preferred_element_type=f32) for a, d, b in zip(qk, decay, uwb)]
    gl = [g[c - 1:c, :] for g in gc_col]
    k_dec = [k * jnp.exp(l - g) for k, l, g in zip(kc, gl, gc_col)]
    kw = [lax.dot_general(kd.astype(bf16), b, tn, preferred_element_type=f32) for kd, b in zip(k_dec, uwb)]
    for ci, sl in enumerate(chunks):
        qp_ref[sl, :] = qc[ci] * eg[ci] - iw[ci][:, :c]
        op_ref[sl, :] = iw[ci][:, c:]
        wm_ref[sl, :] = kw[ci][:, :c]
        nm_ref[sl, :] = kw[ci][:, c:]
        dg_ref[ci * SUBLANES:(ci + 1) * SUBLANES, :] = jnp.broadcast_to(jnp.exp(gl[ci]), (SUBLANES, c))


def _dn_prep(proj, conv_w, gates4, batch, length, heads, cpb):
    c = LANES
    rows = cpb * c
    nrb = length // rows
    hb8 = rows // SUBLANES

    def xspec(off):
        return pl.BlockSpec((rows, c), lambda b, h, r: (b * nrb + r, off + h))

    def hspec(off):
        return pl.BlockSpec(
            (SUBLANES, c),
            lambda b, h, r: (jnp.maximum((b * nrb + r) * hb8 - 1, 0), off + h))

    def wspec(off):
        return pl.BlockSpec((CONV_WIDTH, c), lambda b, h, r: (0, off + h))

    def gspec(off):
        return pl.BlockSpec((None, None, 1, rows), lambda b, h, r: (b, off + h, 0, r))

    big = jax.ShapeDtypeStruct((batch, heads, length, c), f32)
    ospec = pl.BlockSpec((None, None, rows, c), lambda b, h, r: (b, h, r, 0))
    dg_shape = jax.ShapeDtypeStruct((batch, heads, length // c * SUBLANES, c), f32)
    dg_spec = pl.BlockSpec((None, None, cpb * SUBLANES, c), lambda b, h, r: (b, h, r, 0))
    return pl.pallas_call(
        functools.partial(_dn_prep_kernel, cpb=cpb),
        grid=(batch, heads, nrb),
        in_specs=[xspec(0), xspec(heads), xspec(2 * heads),
                  hspec(0), hspec(heads), hspec(2 * heads),
                  wspec(0), wspec(heads), wspec(2 * heads),
                  gspec(0), gspec(heads)],
        out_specs=[ospec, ospec, ospec, ospec, dg_spec],
        out_shape=[big, big, big, big, dg_shape],
        compiler_params=_params(("parallel", "parallel", "parallel"), VMEM_LIMIT_BYTES),
        name="dn_prep",
    )(proj, proj, proj, proj, proj, proj, conv_w, conv_w, conv_w, gates4, gates4)


def _dn_seq_kernel(qp_ref, op_ref, wm_ref, nm_ref, dg_ref, z_ref, nw_ref, o_ref, s_scr, *, hb):
    c = LANES

    @pl.when(pl.program_id(1) == 0)
    def _():
        s_scr[...] = jnp.zeros_like(s_scr)

    for i in range(hb):
        s = s_scr[i]
        s_b = s.astype(bf16)
        o = jnp.dot(qp_ref[i].astype(bf16), s_b, preferred_element_type=f32) + op_ref[i]
        s_scr[i] = (dg_ref[i][0:1, :] * s
                    - jnp.dot(wm_ref[i].astype(bf16), s_b, preferred_element_type=f32) + nm_ref[i])
        o = o * lax.rsqrt(jnp.mean(o * o, axis=-1, keepdims=True) + NORM_EPS)
        o = o * nw_ref[...] * _silu(z_ref[:, i * c:(i + 1) * c])
        o_ref[:, i * c:(i + 1) * c] = o.astype(o_ref.dtype)


def _dn_seq(qp, op, wm, nm, dg, proj, norm_w, batch, length, heads, hb, z_col_block):
    c = LANES
    nc = length // c
    hpb = heads // hb
    bh = batch * heads

    def cspec(rows):
        return pl.BlockSpec((hb, rows, c), lambda g, t: (g, t, 0))

    flat = lambda a: a.reshape(bh, a.shape[2], c)
    return pl.pallas_call(
        functools.partial(_dn_seq_kernel, hb=hb),
        grid=(bh // hb, nc),
        in_specs=[cspec(c), cspec(c), cspec(c), cspec(c), cspec(SUBLANES),
                  pl.BlockSpec((c, hb * c), lambda g, t: ((g // hpb) * nc + t, z_col_block + g % hpb)),
                  pl.BlockSpec((1, c), lambda g, t: (0, 0))],
        out_specs=pl.BlockSpec((c, hb * c), lambda g, t: ((g // hpb) * nc + t, g % hpb)),
        out_shape=jax.ShapeDtypeStruct((batch * length, heads * c), bf16),
        scratch_shapes=[pltpu.VMEM((hb, c, c), f32)],
        compiler_params=_params(("parallel", "arbitrary"), VMEM_LIMIT_BYTES),
        name="dn_seq",
    )(flat(qp), flat(op), flat(wm), flat(nm), flat(dg), proj, norm_w.reshape(1, c).astype(f32))


def _s5_disc_kernel(ar_ref, ai_ref, ldt_ref, br_ref, bi_ref, lr_ref, li_ref, bbr_ref, bbi_ref):
    ar, ai = ar_ref[...], ai_ref[...]
    dt = jnp.exp(ldt_ref[...])
    mag = jnp.exp(ar * dt)
    lb_re, lb_im = mag * jnp.cos(ai * dt), mag * jnp.sin(ai * dt)
    den = ar * ar + ai * ai
    nr, ni = lb_re - 1.0, lb_im
    f_re = (nr * ar + ni * ai) / den
    f_im = (ni * ar - nr * ai) / den
    lr_ref[...] = lb_re
    li_ref[...] = lb_im
    br, bi = br_ref[...], bi_ref[...]
    bbr_ref[...] = f_re * br - f_im * bi
    bbi_ref[...] = f_re * bi + f_im * br


def _s5_discretise(a_re, a_im, log_dt, b_re_t, b_im_t):
    g, n = a_re.shape
    p = b_re_t.shape[1]
    return pl.pallas_call(
        _s5_disc_kernel,
        out_shape=[jax.ShapeDtypeStruct((g, 1, n), f32), jax.ShapeDtypeStruct((g, 1, n), f32),
                   jax.ShapeDtypeStruct((g, p, n), f32), jax.ShapeDtypeStruct((g, p, n), f32)],
        name="s5_discretise",
    )(a_re.reshape(g, 1, n).astype(f32), a_im.reshape(g, 1, n).astype(f32),
      log_dt.reshape(g, 1, 1).astype(f32), b_re_t.astype(f32), b_im_t.astype(f32))


def _s5_scan_kernel(u_ref, bm_ref, cre_ref, cim_ref, lr_ref, li_ref, d_ref, o_ref,
                    bu_scr, hs_scr, st_scr, *, nb, tt, sh):
    nq = 2 * sh // LANES

    @pl.when(pl.program_id(2) == 0)
    def _():
        st_scr[...] = jnp.zeros_like(st_scr)

    u = u_ref[...]
    for j in range(nb):
        bu = _dot1(u[:, j * LANES:(j + 1) * LANES], bm_ref[j])
        for q in range(nq):
            bu_scr[q, j * tt:(j + 1) * tt, :] = bu[:, q * LANES:(q + 1) * LANES]
    lr, li = lr_ref[...], li_ref[...]

    def body(t, carry):
        hr, hi = carry
        bu = jnp.concatenate([bu_scr[q, pl.ds(t, nb, stride=tt), :] for q in range(nq)], axis=1)
        nr = lr * hr - li * hi + bu[:, :sh]
        ni = lr * hi + li * hr + bu[:, sh:]
        for q in range(nq // 2):
            hs_scr[q, pl.ds(t, nb, stride=tt), :] = nr[:, q * LANES:(q + 1) * LANES]
            hs_scr[nq // 2 + q, pl.ds(t, nb, stride=tt), :] = ni[:, q * LANES:(q + 1) * LANES]
        return nr, ni

    hr, hi = lax.fori_loop(0, tt, body, (st_scr[0], st_scr[1]))
    st_scr[0] = hr
    st_scr[1] = hi
    for j in range(nb):
        h_re = jnp.concatenate([hs_scr[q, j * tt:(j + 1) * tt, :] for q in range(nq // 2)], axis=1)
        h_im = jnp.concatenate([hs_scr[nq // 2 + q, j * tt:(j + 1) * tt, :] for q in range(nq // 2)], axis=1)
        y = (_dot1(h_re, cre_ref[j]) - _dot1(h_im, cim_ref[j])
             + d_ref[:, j * LANES:(j + 1) * LANES] * u[:, j * LANES:(j + 1) * LANES])
        o_ref[:, j * LANES:(j + 1) * LANES] = _gelu(y)


def _s5_scan(proj, u_col_block, bm, cre, cim, lr, li, d, batch, length, nb, tt):
    nblk, _, sh2 = bm.shape
    sh = sh2 // 2
    nh = nblk // nb
    nt = length // tt
    width = nb * LANES
    return pl.pallas_call(
        functools.partial(_s5_scan_kernel, nb=nb, tt=tt, sh=sh),
        grid=(batch, nh, nt),
        in_specs=[
            pl.BlockSpec((tt, width), lambda b, h, t: (b * nt + t, u_col_block + h)),
            pl.BlockSpec((nb, LANES, sh2), lambda b, h, t: (h, 0, 0)),
            pl.BlockSpec((nb, sh, LANES), lambda b, h, t: (h, 0, 0)),
            pl.BlockSpec((nb, sh, LANES), lambda b, h, t: (h, 0, 0)),
            pl.BlockSpec((nb, sh), lambda b, h, t: (h, 0)),
            pl.BlockSpec((nb, sh), lambda b, h, t: (h, 0)),
            pl.BlockSpec((1, width), lambda b, h, t: (0, h)),
        ],
        out_specs=pl.BlockSpec((tt, width), lambda b, h, t: (b * nt + t, h)),
        out_shape=jax.ShapeDtypeStruct((batch * length, nblk * LANES), f32),
        scratch_shapes=[pltpu.VMEM((sh2 // LANES, nb * tt, LANES), f32),
                        pltpu.VMEM((sh2 // LANES, nb * tt, LANES), f32),
                        pltpu.VMEM((2, nb, sh), f32)],
        compiler_params=_params(("parallel", "parallel", "arbitrary"), VMEM_LIMIT_BYTES),
        name="s5_scan",
    )(proj, bm, cre, cim, lr, li, d.reshape(1, -1).astype(f32))


def _glu_norm_kernel(y_ref, w_ref, b_ref, nw_ref, o_ref):
    y = y_ref[...]
    gate = jnp.dot(y.astype(bf16), w_ref[...], preferred_element_type=f32) + b_ref[...]
    y = y * jax.nn.sigmoid(gate)
    ms = jnp.mean(y * y, axis=-1, keepdims=True)
    o_ref[...] = (y * lax.rsqrt(ms + NORM_EPS) * nw_ref[...]).astype(o_ref.dtype)


def _glu_norm(y, w_bf, b, nw, tm):
    t, n = y.shape
    row = lambda a: a.reshape(1, n).astype(f32)
    return pl.pallas_call(
        _glu_norm_kernel,
        grid=(t // tm,),
        in_specs=[pl.BlockSpec((tm, n), lambda i: (i, 0)), pl.BlockSpec((n, n), lambda i: (0, 0)),
                  pl.BlockSpec((1, n), lambda i: (0, 0)), pl.BlockSpec((1, n), lambda i: (0, 0))],
        out_specs=pl.BlockSpec((tm, n), lambda i: (i, 0)),
        out_shape=jax.ShapeDtypeStruct((t, n), bf16),
        compiler_params=_params(("parallel",), VMEM_LIMIT_BYTES),
        name="s5_glu_norm",
    )(y, w_bf, row(b), row(nw))


def _out_proj_kernel(a_ref, b_ref, wa_ref, wb_ref, x_ref, o_ref):
    o_ref[...] = (x_ref[...]
                  + jnp.dot(a_ref[...], wa_ref[...], preferred_element_type=f32)
                  + jnp.dot(b_ref[...], wb_ref[...], preferred_element_type=f32))


def _out_proj(mix_a, mix_b, wa, wb, x2d, tm, tn):
    t, d = x2d.shape
    ka, kb = mix_a.shape[1], mix_b.shape[1]
    return pl.pallas_call(
        _out_proj_kernel,
        grid=(t // tm, d // tn),
        in_specs=[pl.BlockSpec((tm, ka), lambda i, j: (i, 0)), pl.BlockSpec((tm, kb), lambda i, j: (i, 0)),
                  pl.BlockSpec((ka, tn), lambda i, j: (0, j)), pl.BlockSpec((kb, tn), lambda i, j: (0, j)),
                  pl.BlockSpec((tm, tn), lambda i, j: (i, j))],
        out_specs=pl.BlockSpec((tm, tn), lambda i, j: (i, j)),
        out_shape=jax.ShapeDtypeStruct((t, d), f32),
        compiler_params=_params(("parallel", "parallel"), VMEM_LIMIT_BYTES),
        name="out_proj",
    )(mix_a, mix_b, wa, wb, x2d)


def _peer_route_kernel(q_ref, keys_ref, thr_ref, e1_ref, s2_ref, e2_ref, top_scr, cand_scr, *, n_cand):
    neg = -jnp.inf
    k = PEER_TOPK
    nt = (((1,), (1,)), ((), ()))
    scores = []
    for half in range(2):
        s = _dot3(keys_ref[half], q_ref[:, half * LANES:(half + 1) * LANES], nt)
        scores.append(s)
        work = s
        for i in range(k):
            m = jnp.max(work, axis=0, keepdims=True)
            top_scr[half, i:i + 1, :] = m
            work = jnp.where(work >= m, neg, work)
    top1, top2 = top_scr[0], top_scr[1]
    cand_scr[...] = jnp.full(cand_scr.shape, neg, f32)
    off = 0
    for i in range(k):
        n_i = k // (i + 1)
        cand_scr[off:off + n_i, :] = top1[i:i + 1, :] + top2[0:n_i, :]
        off += n_i
    assert off == n_cand
    cand = cand_scr[...]
    work = cand
    best = None
    m = None
    for i in range(k):
        m = jnp.max(work, axis=0, keepdims=True)
        if i == 0:
            best = m
        work = jnp.where(work >= m, neg, work)
    theta = m
    z = jnp.sum(jnp.where(cand >= theta, jnp.exp(cand - best), 0.0), axis=0, keepdims=True)
    thr_ref[...] = theta - scores[0]
    e1_ref[...] = jnp.exp(scores[0] - top1[0:1, :]) / z
    s2_ref[...] = scores[1]
    e2_ref[...] = jnp.exp(scores[1] - top2[0:1, :])


def _peer_route(q, sub_keys, heads, tm):
    t = q.shape[0]
    nk, half = sub_keys.shape[1], sub_keys.shape[2]
    assert nk == LANES and half == LANES
    n_cand = sum(PEER_TOPK // (i + 1) for i in range(PEER_TOPK))
    cand_rows = -(-n_cand // SUBLANES) * SUBLANES
    out = jax.ShapeDtypeStruct((heads, nk, t), f32)
    ospec = pl.BlockSpec((None, nk, tm), lambda i, h: (h, 0, i))
    return pl.pallas_call(
        functools.partial(_peer_route_kernel, n_cand=n_cand),
        grid=(t // tm, heads),
        in_specs=[pl.BlockSpec((tm, 2 * half), lambda i, h: (i, h)),
                  pl.BlockSpec((2, nk, half), lambda i, h: (0, 0, 0))],
        out_specs=[ospec, ospec, ospec, ospec],
        out_shape=[out, out, out, out],
        scratch_shapes=[pltpu.VMEM((2, PEER_TOPK, tm), f32), pltpu.VMEM((cand_rows, tm), f32)],
        compiler_params=_params(("parallel", "parallel"), VMEM_LIMIT_BYTES),
        name="peer_route",
    )(q, sub_keys.astype(f32))


def _peer_dense_kernel(x_ref, u_ref, vt_ref, thr_ref, e1_ref, s2_ref, e2_ref, o_ref, p0_scr, p1_scr, gate_scr,
                       *, heads, na):
    e = pl.program_id(1)

    @pl.when(e == 0)
    def _():
        o_ref[...] = jnp.zeros_like(o_ref)
        p1_scr[...] = jnp.zeros_like(p1_scr)

    def step(p_new, p_old):
        nt = (((1,), (1,)), ((), ()))
        a0 = jnp.minimum(e, pl.num_programs(1) - 2) * na
        for al in range(na):
            sl = slice(al * LANES, (al + 1) * LANES)
            for j in range(o_ref.shape[1] // LANES):
                cols = slice(j * LANES, (j + 1) * LANES)
                gate = None
                for h in range(heads):
                    thr = thr_ref[h, pl.ds(a0 + al, 1), :][:, cols]
                    e1 = e1_ref[h, pl.ds(a0 + al, 1), :][:, cols]
                    term = jnp.where(s2_ref[h, :, cols] >= thr, e2_ref[h, :, cols], 0.0) * e1
                    gate = term if gate is None else gate + term
                gate_scr[sl, cols] = gate
        act = lax.dot_general(u_ref[...], x_ref[...], nt, preferred_element_type=f32)
        p_new[...] = (gate_scr[...] * _gelu(act)).astype(bf16)
        o_ref[...] += jnp.dot(vt_ref[...], p_old[...], preferred_element_type=f32)

    @pl.when(e % 2 == 0)
    def _():
        step(p0_scr, p1_scr)

    @pl.when(e % 2 == 1)
    def _():
        step(p1_scr, p0_scr)


def _peer_dense(xn, u_bf, vt_bf, thr, e1, s2, e2, tm, te):
    t, d = xn.shape
    ne = u_bf.shape[0] // te
    heads, nk, _ = thr.shape
    na = te // nk
    once = pl.Buffered(1)
    rspec = pl.BlockSpec((heads, nk, tm), lambda i, e: (0, 0, i), pipeline_mode=once)
    return pl.pallas_call(
        functools.partial(_peer_dense_kernel, heads=heads, na=na),
        grid=(t // tm, ne + 1),
        in_specs=[pl.BlockSpec((tm, d), lambda i, e: (i, 0), pipeline_mode=once),
                  pl.BlockSpec((te, d), lambda i, e: (jnp.minimum(e, ne - 1), 0)),
                  pl.BlockSpec((d, te), lambda i, e: (0, jnp.maximum(e - 1, 0))),
                  rspec, rspec, rspec, rspec],
        out_specs=pl.BlockSpec((d, tm), lambda i, e: (0, i)),
        out_shape=jax.ShapeDtypeStruct((d, t), f32),
        scratch_shapes=[pltpu.VMEM((te, tm), bf16), pltpu.VMEM((te, tm), bf16), pltpu.VMEM((te, tm), f32)],
        compiler_params=_params(("parallel", "arbitrary"), VMEM_LIMIT_BYTES),
        name="peer_dense",
    )(xn, u_bf, vt_bf, thr, e1, s2, e2)


def _final_kernel(h_ref, pt_ref, w_ref, o_ref):
    x = h_ref[...] + pt_ref[...].T
    ms = jnp.mean(x * x, axis=-1, keepdims=True)
    o_ref[...] = x * lax.rsqrt(ms + NORM_EPS) * w_ref[...]


def _final(h, peer_t, w, tm):
    t, d = h.shape
    return pl.pallas_call(
        _final_kernel,
        grid=(t // tm,),
        in_specs=[pl.BlockSpec((tm, d), lambda i: (i, 0)), pl.BlockSpec((d, tm), lambda i: (0, i)),
                  pl.BlockSpec((1, d), lambda i: (0, 0))],
        out_specs=pl.BlockSpec((tm, d), lambda i: (i, 0)),
        out_shape=jax.ShapeDtypeStruct((t, d), f32),
        compiler_params=_params(("parallel",), VMEM_LIMIT_BYTES),
        name="final_norm",
    )(h, peer_t, w.reshape(1, d).astype(f32))


def _block_diag(w, groups_per_block):
    g, a, b = w.shape
    nblk = g // groups_per_block
    w = w.reshape(nblk, groups_per_block, a, b)
    eye = jnp.eye(groups_per_block, dtype=w.dtype)
    return jnp.einsum("kgab,gh->kgahb", w, eye).reshape(nblk, groups_per_block * a, groups_per_block * b)


def _layer(h2d, batch, length, attn_norm_w, w_in, conv_w, dt_bias, a_log, dn_norm_w,
           a_re, a_im, log_dt, b_re, b_im, c_re, c_im, ssm_d, glu_w, glu_b, ssm_norm_w,
           w_out, ffn_norm_w, w_query, sub_keys, peer_u, peer_v, final_w):
    t, d = h2d.shape
    heads = dt_bias.shape[0]
    c = LANES
    dnw = heads * c
    ssmw = ssm_d.shape[0]
    groups, nstate = a_re.shape
    gsize = b_re.shape[-1]
    gpb = c // gsize
    assert dn_norm_w.shape[0] == c and 2 * heads <= c and groups * gsize == ssmw

    z_end = 4 * dnw
    w_ba = jnp.pad(w_in[:, z_end:z_end + 2 * heads], ((0, 0), (0, c - 2 * heads)))
    w_cat = jnp.concatenate([w_in[:, :z_end], w_in[:, z_end + 2 * heads:], w_ba], axis=1).astype(bf16)
    ncols = w_cat.shape[1]
    xn = _rmsnorm(h2d, attn_norm_w, bf16, _pick_tile(t, 256, SUBLANES))
    proj = _matmul(xn, w_cat, _pick_tile(t, 1024, SUBLANES), _pick_tile(ncols, 1152), "in_proj")

    gates = _dn_gates(proj, (z_end + ssmw) // c, a_log, dt_bias, batch, length, _pick_tile(length, 512))
    gates4 = gates.reshape(batch, 2 * heads, 1, length)
    cpb = _pick_tile(length // c, 4, 1)
    qp, op, wm, nm, dg = _dn_prep(proj, conv_w.astype(f32), gates4, batch, length, heads, cpb)
    hb = _pick_tile(heads, 8, 1)
    mix_a = _dn_seq(qp, op, wm, nm, dg, proj, dn_norm_w, batch, length, heads, hb, 3 * heads // hb)

    lr, li, bbr, bbi = _s5_discretise(a_re, a_im, log_dt,
                                      jnp.swapaxes(b_re, 1, 2), jnp.swapaxes(b_im, 1, 2))
    nblk = groups // gpb
    sh = gpb * nstate
    bm = jnp.concatenate([_block_diag(bbr, gpb), _block_diag(bbi, gpb)], axis=2).astype(bf16)
    cre = _block_diag(jnp.swapaxes(c_re, 1, 2).astype(f32), gpb).astype(bf16)
    cim = _block_diag(jnp.swapaxes(c_im, 1, 2).astype(f32), gpb).astype(bf16)
    nb = _pick_tile(nblk, SUBLANES, 1)
    y = _s5_scan(proj, z_end // (nb * c), bm, cre, cim, lr.reshape(nblk, sh), li.reshape(nblk, sh),
                 ssm_d, batch, length, nb, _pick_tile(length, 256, SUBLANES))
    mix_b = _glu_norm(y, glu_w.astype(bf16), glu_b, ssm_norm_w, _pick_tile(t, 512, SUBLANES))

    w_out_bf = w_out.astype(bf16)
    h1 = _out_proj(mix_a, mix_b, w_out_bf[:dnw], w_out_bf[dnw:], h2d,
                   _pick_tile(t, 512, SUBLANES), _pick_tile(d, 1024))

    xn2 = _rmsnorm(h1, ffn_norm_w, bf16, _pick_tile(t, 256, SUBLANES))
    qd = w_query.shape[1]
    q = _matmul(xn2, w_query.astype(bf16), _pick_tile(t, 1024, SUBLANES), _pick_tile(qd, 1024), "peer_query")
    pheads = qd // (2 * sub_keys.shape[-1])
    tm = _pick_tile(t, 512)
    thr, e1, s2, e2 = _peer_route(q, sub_keys, pheads, tm)
    peer_t = _peer_dense(xn2, peer_u.astype(bf16), peer_v.T.astype(bf16), thr, e1, s2, e2,
                         tm, _pick_tile(peer_u.shape[0], 512))
    return _final(h1, peer_t, final_w, _pick_tile(t, 256))


def kernel(x, attn_norm_w, w_in, conv_w, dn_dt_bias, dn_a_log, dn_norm_w, ssm_a_re, ssm_a_im, ssm_log_dt, ssm_b_re, ssm_b_im, ssm_c_re, ssm_c_im, ssm_d, ssm_glu_w, ssm_glu_b, ssm_norm_w, w_out, ffn_norm_w, peer_w_query, peer_sub_keys, peer_u, peer_v, final_norm_w):
    batch, length, d = x.shape
    assert attn_norm_w.shape[0] == 1, "one layer; the final norm is fused after its PEER residual"
    out = _layer(x.reshape(batch * length, d), batch, length,
                 attn_norm_w[0], w_in[0], conv_w[0], dn_dt_bias[0], dn_a_log[0], dn_norm_w[0],
                 ssm_a_re[0], ssm_a_im[0], ssm_log_dt[0], ssm_b_re[0], ssm_b_im[0], ssm_c_re[0],
                 ssm_c_im[0], ssm_d[0], ssm_glu_w[0], ssm_glu_b[0], ssm_norm_w[0], w_out[0],
                 ffn_norm_w[0], peer_w_query[0], peer_sub_keys[0], peer_u[0], peer_v[0], final_norm_w)
    return out.reshape(batch, length, d)
```
